```python
import jax, jax.numpy as jnp
from jax import lax
import numpy as np

D_MODEL = 2048
BATCH = 2
SEQ = 8192
DEPTH = 1

CHUNK = 64
Q_BLOCK = 128
EPS = 1e-6
D_FF = 5632

GLA_HEADS = 4
GLA_KEY = D_MODEL // 2
GLA_VAL = D_MODEL
GLA_DK = GLA_KEY // GLA_HEADS
GLA_DV = GLA_VAL // GLA_HEADS
GLA_GATE_RANK = 16
GLA_GATE_NORMALIZER = 16.0

FOX_DH = 128
FOX_HEADS = D_MODEL // FOX_DH
FOX_W = FOX_HEADS * FOX_DH

IN_SIZES = (GLA_KEY, GLA_KEY, GLA_VAL, GLA_VAL, GLA_GATE_RANK,
            FOX_W, FOX_W, FOX_W, FOX_HEADS, FOX_W, D_MODEL, D_MODEL)
IN_WIDTH = sum(IN_SIZES)

kernel_name = "hybrid_gla_fox_macaron_block"


def rmsnorm(x, g):
    x32 = x.astype(jnp.float32)
    r = x32 * lax.rsqrt(jnp.mean(x32 * x32, axis=-1, keepdims=True) + EPS)
    return (r * g.astype(jnp.float32)).astype(x.dtype)


def swiglu(x, w_in, w_out):
    gu = x @ w_in
    g, u = gu[..., :D_FF], gu[..., D_FF:]
    return (jax.nn.silu(g) * u) @ w_out


def split_columns(p):
    parts, off = [], 0
    for n in IN_SIZES:
        parts.append(p[..., off:off + n])
        off += n
    return parts


def gla_branch(q, k, v, r, a_lr, w_a_up, b_a, g_onorm):
    B, T, _ = q.shape
    nc = T // CHUNK
    q = q.reshape(B, nc, CHUNK, GLA_HEADS, GLA_DK) * (GLA_DK ** -0.5)
    k = k.reshape(B, nc, CHUNK, GLA_HEADS, GLA_DK)
    v = v.reshape(B, nc, CHUNK, GLA_HEADS, GLA_DV)
    log_a = jax.nn.log_sigmoid((a_lr @ w_a_up + b_a).astype(jnp.float32)) / GLA_GATE_NORMALIZER
    G = jnp.cumsum(log_a.reshape(B, nc, CHUNK, GLA_HEADS, GLA_DK), axis=2)
    G_end = G[:, :, -1]
    k_dec = (k * jnp.exp(G_end[:, :, None] - G)).astype(k.dtype)
    a_chunk = jnp.exp(G_end).astype(k.dtype)

    def step(S, inp):
        q_c, k_c, v_c, a_c = inp
        S = a_c[..., None] * S + jnp.einsum("bchk,bchv->bhkv", k_c, v_c)
        o_c = jnp.einsum("bchk,bhkv->bchv", q_c, S)
        return S, o_c

    xs = (jnp.moveaxis(q, 1, 0), jnp.moveaxis(k_dec, 1, 0),
          jnp.moveaxis(v, 1, 0), jnp.moveaxis(a_chunk, 1, 0))
    S0 = jnp.zeros((B, GLA_HEADS, GLA_DK, GLA_DV), jnp.float32)
    _, o = lax.scan(step, S0, xs)
    o = jnp.moveaxis(o, 0, 1).reshape(B, T, GLA_HEADS, GLA_DV).astype(v.dtype)
    o = rmsnorm(o, g_onorm)
    return o.reshape(B, T, GLA_VAL) * jax.nn.silu(r)


def fox_branch(q, k, v, f_logit, og, b_f, g_q, g_k):
    B, T, _ = q.shape
    q = rmsnorm(q.reshape(B, T, FOX_HEADS, FOX_DH), g_q) * (FOX_DH ** -0.5)
    k = rmsnorm(k.reshape(B, T, FOX_HEADS, FOX_DH), g_k)
    v = v.reshape(B, T, FOX_HEADS, FOX_DH)
    log_f = jax.nn.log_sigmoid(f_logit.astype(jnp.float32) + b_f.astype(jnp.float32))
    F = jnp.transpose(jnp.cumsum(log_f, axis=1), (0, 2, 1))
    outs = []
    for i in range(T // Q_BLOCK):
        q0, q1 = i * Q_BLOCK, (i + 1) * Q_BLOCK
        s = jnp.einsum("bqhd,bkhd->bhqk", q[:, q0:q1], k[:, :q1]).astype(jnp.float32)
        s = s + F[:, :, q0:q1, None] - F[:, :, None, :q1]
        q_idx = q0 + jnp.arange(Q_BLOCK)
        k_idx = jnp.arange(q1)
        s = jnp.where(k_idx[None, :] <= q_idx[:, None], s, -jnp.inf)
        p = jax.nn.softmax(s, axis=-1).astype(v.dtype)
        outs.append(jnp.einsum("bhqk,bkhd->bqhd", p, v[:, :q1]))
    o = jnp.concatenate(outs, axis=1).reshape(B, T, FOX_W)
    return o * jax.nn.sigmoid(og)


def mixer(h, w_in, gla_w_a_up, gla_b_a, gla_onorm, fox_b_f, fox_q_norm, fox_k_norm,
          w_br_gla, w_br_fox, b_merge, w_out):
    (g_q, g_k, g_v, g_r, g_alr, f_q, f_k, f_v, f_f, f_og, m_gla, m_fox) = split_columns(h @ w_in)
    y_gla = gla_branch(g_q, g_k, g_v, g_r, g_alr, gla_w_a_up, gla_b_a, gla_onorm) @ w_br_gla
    y_fox = fox_branch(f_q, f_k, f_v, f_f, f_og, fox_b_f, fox_q_norm, fox_k_norm) @ w_br_fox
    merged = jax.nn.sigmoid(m_gla + b_merge[0]) * y_gla + jax.nn.sigmoid(m_fox + b_merge[1]) * y_fox
    return merged @ w_out


def setup_inputs(seed: int = 0) -> dict:
    key = jax.random.key(seed)
    ks = jax.random.split(key, 24)

    def w(k, shape, fan_in):
        return jax.random.normal(k, shape, jnp.float32) * (fan_in ** -0.5)

    def gain(k, shape):
        return 1.0 + 0.02 * jax.random.normal(k, shape, jnp.float32)

    def small(k, shape):
        return 0.01 * jax.random.normal(k, shape, jnp.float32)

    L = DEPTH
    return {
        "x": jax.random.normal(ks[0], (BATCH, SEQ, D_MODEL), jnp.float32),
        "norm_ffn1": gain(ks[1], (L, D_MODEL)),
        "ffn1_w_in": w(ks[2], (L, D_MODEL, 2 * D_FF), D_MODEL),
        "ffn1_w_out": w(ks[3], (L, D_FF, D_MODEL), D_FF),
        "norm_mix": gain(ks[4], (L, D_MODEL)),
        "w_in": w(ks[5], (L, D_MODEL, IN_WIDTH), D_MODEL),
        "gla_w_a_up": w(ks[6], (L, GLA_GATE_RANK, GLA_KEY), GLA_GATE_RANK),
        "gla_b_a": small(ks[7], (L, GLA_KEY)),
        "gla_onorm": gain(ks[8], (L, GLA_HEADS, GLA_DV)),
        "fox_b_f": jax.random.uniform(ks[9], (L, FOX_HEADS), jnp.float32, 0.0, 4.0),
        "fox_q_norm": gain(ks[10], (L, FOX_DH)),
        "fox_k_norm": gain(ks[11], (L, FOX_DH)),
        "w_br_gla": w(ks[12], (L, GLA_VAL, D_MODEL), GLA_VAL),
        "w_br_fox": w(ks[13], (L, FOX_W, D_MODEL), FOX_W),
        "b_merge": small(ks[14], (L, 2, D_MODEL)),
        "w_out": w(ks[15], (L, D_MODEL, D_MODEL), D_MODEL),
        "norm_ffn2": gain(ks[16], (L, D_MODEL)),
        "ffn2_w_in": w(ks[17], (L, D_MODEL, 2 * D_FF), D_MODEL),
        "ffn2_w_out": w(ks[18], (L, D_FF, D_MODEL), D_FF),
        "norm_final": gain(ks[19], (D_MODEL,)),
    }


def reference(x, norm_ffn1, ffn1_w_in, ffn1_w_out, norm_mix, w_in, gla_w_a_up, gla_b_a,
              gla_onorm, fox_b_f, fox_q_norm, fox_k_norm, w_br_gla, w_br_fox, b_merge,
              w_out, norm_ffn2, ffn2_w_in, ffn2_w_out, norm_final):
    for l in range(DEPTH):
        x = x + 0.5 * swiglu(rmsnorm(x, norm_ffn1[l]), ffn1_w_in[l], ffn1_w_out[l])
        x = x + mixer(rmsnorm(x, norm_mix[l]), w_in[l], gla_w_a_up[l], gla_b_a[l], gla_onorm[l],
                      fox_b_f[l], fox_q_norm[l], fox_k_norm[l], w_br_gla[l], w_br_fox[l],
                      b_merge[l], w_out[l])
        x = x + 0.5 * swiglu(rmsnorm(x, norm_ffn2[l]), ffn2_w_in[l], ffn2_w_out[l])
    return rmsnorm(x, norm_final)
```

```python
import functools

import jax
import jax.numpy as jnp
from jax import lax
from jax.experimental import pallas as pl
from jax.experimental.pallas import tpu as pltpu

F32 = jnp.float32
BF16 = jnp.bfloat16

EPS = 1e-6
CHUNK = 64
GLA_HEADS = 4
GLA_GATE_RANK = 16
GLA_GATE_NORMALIZER = 16.0
FOX_DH = 128
LANES = 128
VMEM_LIMIT = 56 * 1024 * 1024


def _cparams(sem):
    return pltpu.CompilerParams(dimension_semantics=sem, vmem_limit_bytes=VMEM_LIMIT)


def _rms(x, g):
    return x * lax.rsqrt(jnp.mean(x * x, axis=-1, keepdims=True) + EPS) * g


def _log_sigmoid(x):
    return jnp.minimum(x, 0.0) - jnp.log1p(jnp.exp(-jnp.abs(x)))


def _split3(x):
    hi = x.astype(BF16)
    r1 = x - hi.astype(F32)
    mid = r1.astype(BF16)
    lo = (r1 - mid.astype(F32)).astype(BF16)
    return hi, mid, lo


def _dot(a, b):
    return jnp.dot(a, b, preferred_element_type=F32)


def _dot_nt(a, b):
    return lax.dot_general(a, b, (((1,), (1,)), ((), ())), preferred_element_type=F32)


def _dot_tn(a, b):
    return lax.dot_general(a, b, (((0,), (0,)), ((), ())), preferred_element_type=F32)


def _ffn_kernel(x_ref, g_ref, wg_ref, wu_ref, wo_ref, gpost_ref, *refs, final):
    if final:
        o_ref, h_ref = refs
    else:
        o_ref, hn_ref, h_ref = refs
    f = pl.program_id(1)

    @pl.when(f == 0)
    def _():
        xv = x_ref[...]
        h_ref[...] = _rms(xv, g_ref[...]).astype(BF16)
        o_ref[...] = xv

    h = h_ref[...]
    g = _dot(h, wg_ref[...])
    u = _dot(h, wu_ref[...])
    a = (g * jax.nn.sigmoid(g) * u * 0.5).astype(BF16)
    o_ref[...] += _dot(a, wo_ref[...])

    @pl.when(f == pl.num_programs(1) - 1)
    def _():
        y = _rms(o_ref[...], gpost_ref[...])
        if final:
            o_ref[...] = y
        else:
            hn_ref[...] = y.astype(BF16)


def _ffn(x, g_pre, w_in, w_out, g_post, *, final, tm, tf):
    n, d = x.shape
    dff = w_out.shape[0]
    nf = dff // tf
    out_shape = [jax.ShapeDtypeStruct((n, d), F32)]
    out_specs = [pl.BlockSpec((tm, d), lambda i, f: (i, 0))]
    if not final:
        out_shape.append(jax.ShapeDtypeStruct((n, d), BF16))
        out_specs.append(pl.BlockSpec((tm, d), lambda i, f: (i, 0)))
    res = pl.pallas_call(
        functools.partial(_ffn_kernel, final=final),
        grid=(n // tm, nf),
        in_specs=[
            pl.BlockSpec((tm, d), lambda i, f: (i, 0)),
            pl.BlockSpec((1, d), lambda i, f: (0, 0)),
            pl.BlockSpec((d, tf), lambda i, f: (0, f)),
            pl.BlockSpec((d, tf), lambda i, f: (0, f + nf)),
            pl.BlockSpec((tf, d), lambda i, f: (f, 0)),
            pl.BlockSpec((1, d), lambda i, f: (0, 0)),
        ],
        out_specs=out_specs,
        out_shape=out_shape,
        scratch_shapes=[pltpu.VMEM((tm, d), BF16)],
        compiler_params=_cparams(("parallel", "arbitrary")),
        name="ffn_final" if final else "ffn_mid",
    )(x, g_pre.reshape(1, d), w_in, w_in, w_out, g_post.reshape(1, d))
    return res[0] if final else (res[0], res[1])


def _proj_kernel(h_ref, w_ref, ws_ref, wst_ref, gq_ref, gk_ref, p_ref, s_ref, st_ref,
                 *, tn, jq, jk, nj_sec, qscale):
    j = pl.program_id(1)
    h = h_ref[...]

    @pl.when(j == 0)
    def _():
        s_ref[...] = _dot(h, ws_ref[...])
        st_ref[...] = _dot_nt(wst_ref[...], h)

    acc = _dot(h, w_ref[...])
    is_q = jnp.logical_and(j >= jq, j < jq + nj_sec)
    is_k = jnp.logical_and(j >= jk, j < jk + nj_sec)

    def head_norm(gain, scale):
        for s in range(tn // FOX_DH):
            blk = acc[:, s * FOX_DH:(s + 1) * FOX_DH]
            y = _rms(blk, gain)
            if scale is not None:
                y = y * scale
            p_ref[:, s * FOX_DH:(s + 1) * FOX_DH] = y.astype(BF16)

    @pl.when(is_q)
    def _():
        head_norm(gq_ref[...], qscale)

    @pl.when(is_k)
    def _():
        head_norm(gk_ref[...], None)

    @pl.when(jnp.logical_not(jnp.logical_or(is_q, is_k)))
    def _():
        p_ref[...] = acc.astype(BF16)


def _proj(h, w_big, w_small, w_small_t, g_q, g_k, *, tm, tn, q_off, k_off, sec_w):
    n, d = h.shape
    width = w_big.shape[1]
    return pl.pallas_call(
        functools.partial(_proj_kernel, tn=tn, jq=q_off // tn, jk=k_off // tn,
                          nj_sec=sec_w // tn, qscale=FOX_DH ** -0.5),
        grid=(n // tm, width // tn),
        in_specs=[
            pl.BlockSpec((tm, d), lambda i, j: (i, 0)),
            pl.BlockSpec((d, tn), lambda i, j: (0, j)),
            pl.BlockSpec((d, LANES), lambda i, j: (0, 0)),
            pl.BlockSpec((2 * GLA_GATE_RANK, d), lambda i, j: (0, 0)),
            pl.BlockSpec((1, FOX_DH), lambda i, j: (0, 0)),
            pl.BlockSpec((1, FOX_DH), lambda i, j: (0, 0)),
        ],
        out_specs=[
            pl.BlockSpec((tm, tn), lambda i, j: (i, j)),
            pl.BlockSpec((tm, LANES), lambda i, j: (i, 0)),
            pl.BlockSpec((2 * GLA_GATE_RANK, tm), lambda i, j: (0, i)),
        ],
        out_shape=[
            jax.ShapeDtypeStruct((n, width), BF16),
            jax.ShapeDtypeStruct((n, LANES), F32),
            jax.ShapeDtypeStruct((2 * GLA_GATE_RANK, n), F32),
        ],
        compiler_params=_cparams(("parallel", "arbitrary")),
        name="proj",
    )(h, w_big, w_small, w_small_t, g_q.reshape(1, FOX_DH), g_k.reshape(1, FOX_DH))


def _fcum_kernel(st_ref, bf_ref, f_ref, carry_ref, *, nh, tb):
    t = pl.program_id(1)

    @pl.when(t == 0)
    def _():
        carry_ref[...] = jnp.zeros_like(carry_ref)

    lf = _log_sigmoid(st_ref[GLA_GATE_RANK:GLA_GATE_RANK + nh, :] + bf_ref[...])
    row = lax.broadcasted_iota(jnp.int32, (tb, tb), 0)
    col = lax.broadcasted_iota(jnp.int32, (tb, tb), 1)
    upper = jnp.where(row <= col, 1.0, 0.0).astype(BF16)
    hi, mid, lo = _split3(lf)
    fblk = _dot(hi, upper) + _dot(mid, upper) + _dot(lo, upper) + carry_ref[...]
    f_ref[...] = fblk
    carry_ref[...] = fblk[:, tb - 1:tb]


def _fcum(s_t, b_f, *, batch, seq, tb):
    nh = b_f.shape[0]
    nt = seq // tb
    return pl.pallas_call(
        functools.partial(_fcum_kernel, nh=nh, tb=tb),
        grid=(batch, nt),
        in_specs=[
            pl.BlockSpec((s_t.shape[0], tb), lambda b, t: (0, b * nt + t)),
            pl.BlockSpec((nh, 1), lambda b, t: (0, 0)),
        ],
        out_specs=pl.BlockSpec((nh, tb), lambda b, t: (0, b * nt + t)),
        out_shape=jax.ShapeDtypeStruct((nh, batch * seq), F32),
        scratch_shapes=[pltpu.VMEM((nh, 1), F32)],
        compiler_params=_cparams(("parallel", "arbitrary")),
        name="fcum",
    )(s_t, b_f.reshape(nh, 1))


def _gla_kernel(q_ref, k_ref, v_ref, r_ref, s_ref, wup_ref, ba_ref, gon_ref, o_ref, st_ref,
                *, rows, dk):
    t = pl.program_id(2)

    @pl.when(t == 0)
    def _():
        st_ref[...] = jnp.zeros_like(st_ref)

    z = _dot(s_ref[...].astype(BF16), wup_ref[...]) + ba_ref[...]
    log_a = _log_sigmoid(z) / GLA_GATE_NORMALIZER
    row = lax.broadcasted_iota(jnp.int32, (rows, rows), 0)
    col = lax.broadcasted_iota(jnp.int32, (rows, rows), 1)
    same_chunk = (row // CHUNK) == (col // CHUNK)
    lower = jnp.where(jnp.logical_and(same_chunk, col <= row), 1.0, 0.0).astype(BF16)
    hi, mid, lo = _split3(log_a)
    gcum = _dot(lower, hi) + _dot(lower, mid) + _dot(lower, lo)

    gain = gon_ref[...]
    for c in range(rows // CHUNK):
        r0, r1 = c * CHUNK, (c + 1) * CHUNK
        g_c = gcum[r0:r1]
        g_end = gcum[r1 - 1:r1]
        k_dec = (k_ref[r0:r1, :].astype(F32) * jnp.exp(g_end - g_c)).astype(BF16)
        st = st_ref[...] * jnp.exp(g_end) + _dot_tn(v_ref[r0:r1, :], k_dec)
        st_ref[...] = st
        o = _dot_nt(q_ref[r0:r1, :], st.astype(BF16)) * (dk ** -0.5)
        o = _rms(o, gain)
        rv = r_ref[r0:r1, :].astype(F32)
        o_ref[r0:r1, :] = (o * (rv * jax.nn.sigmoid(rv))).astype(BF16)


def _gla(p, s_row, w_up_pad, b_a, g_onorm, *, batch, seq, rows, q_off, k_off, v_off, r_off,
         dk, dv):
    n = p.shape[0]
    nt = seq // rows
    row_ix = lambda b, h, t: b * nt + t
    return pl.pallas_call(
        functools.partial(_gla_kernel, rows=rows, dk=dk),
        grid=(batch, GLA_HEADS, nt),
        in_specs=[
            pl.BlockSpec((rows, dk), lambda b, h, t: (row_ix(b, h, t), q_off // dk + h)),
            pl.BlockSpec((rows, dk), lambda b, h, t: (row_ix(b, h, t), k_off // dk + h)),
            pl.BlockSpec((rows, dv), lambda b, h, t: (row_ix(b, h, t), v_off // dv + h)),
            pl.BlockSpec((rows, dv), lambda b, h, t: (row_ix(b, h, t), r_off // dv + h)),
            pl.BlockSpec((rows, LANES), lambda b, h, t: (row_ix(b, h, t), 0)),
            pl.BlockSpec((LANES, dk), lambda b, h, t: (0, h)),
            pl.BlockSpec((1, dk), lambda b, h, t: (0, h)),
            pl.BlockSpec((None, 1, dv), lambda b, h, t: (h, 0, 0)),
        ],
        out_specs=pl.BlockSpec((rows, dv), lambda b, h, t: (row_ix(b, h, t), h)),
        out_shape=jax.ShapeDtypeStruct((n, GLA_HEADS * dv), BF16),
        scratch_shapes=[pltpu.VMEM((dv, dk), F32)],
        compiler_params=_cparams(("parallel", "parallel", "arbitrary")),
        name="gla",
    )(p, p, p, p, s_row, w_up_pad, b_a.reshape(1, -1), g_onorm.reshape(GLA_HEADS, 1, dv))


def _fox_kernel(q_ref, k_ref, v_ref, og_ref, f_ref, o_ref, *, tq):
    i = pl.program_id(2)
    q = q_ref[...]
    qs = pl.multiple_of(i * tq, tq)
    f_q0 = f_ref[:, pl.ds(qs, tq)][:, 0:1]

    def scores(ks):
        k = k_ref[pl.ds(ks, tq), :]
        return _dot_nt(q, k) + (f_q0 - f_ref[:, pl.ds(ks, tq)])

    def update(carry, s, ks):
        m, l, acc = carry
        m_new = jnp.maximum(m, jnp.max(s, axis=-1, keepdims=True))
        alpha = jnp.exp(m - m_new)
        pr = jnp.exp(s - m_new)
        l = alpha * l + jnp.sum(pr, axis=-1, keepdims=True)
        acc = alpha * acc + _dot(pr.astype(BF16), v_ref[pl.ds(ks, tq), :])
        return m_new, l, acc

    def body(kb, carry):
        ks = pl.multiple_of(kb * tq, tq)
        return update(carry, scores(ks), ks)

    init = (jnp.full((tq, 1), -jnp.inf, F32), jnp.zeros((tq, 1), F32),
            jnp.zeros((tq, FOX_DH), F32))
    carry = lax.fori_loop(0, i, body, init)

    row = lax.broadcasted_iota(jnp.int32, (tq, tq), 0)
    col = lax.broadcasted_iota(jnp.int32, (tq, tq), 1)
    s = jnp.where(col <= row, scores(qs), -jnp.inf)
    m, l, acc = update(carry, s, qs)
    og = og_ref[...].astype(F32)
    o_ref[...] = (acc / l * jax.nn.sigmoid(og)).astype(BF16)


def _fox(p, f_cum, *, batch, seq, heads, tq, q_off, k_off, v_off, og_off):
    n = p.shape[0]
    nq = seq // tq
    f3 = f_cum.reshape(heads, 1, n)
    return pl.pallas_call(
        functools.partial(_fox_kernel, tq=tq),
        grid=(batch, heads, nq),
        in_specs=[
            pl.BlockSpec((tq, FOX_DH), lambda b, h, i: (b * nq + i, q_off // FOX_DH + h)),
            pl.BlockSpec((seq, FOX_DH), lambda b, h, i: (b, k_off // FOX_DH + h)),
            pl.BlockSpec((seq, FOX_DH), lambda b, h, i: (b, v_off // FOX_DH + h)),
            pl.BlockSpec((tq, FOX_DH), lambda b, h, i: (b * nq + i, og_off // FOX_DH + h)),
            pl.BlockSpec((None, 1, seq), lambda b, h, i: (h, 0, b)),
        ],
        out_specs=pl.BlockSpec((tq, FOX_DH), lambda b, h, i: (b * nq + i, h)),
        out_shape=jax.ShapeDtypeStruct((n, heads * FOX_DH), BF16),
        compiler_params=_cparams(("parallel", "parallel", "arbitrary")),
        name="fox",
    )(p, p, p, p, f3)


def _merge_kernel(yg_ref, yf_ref, wg_ref, wf_ref, mg_ref, mf_ref, bm_ref, o_ref):
    y_gla = _dot(yg_ref[...], wg_ref[...])
    y_fox = _dot(yf_ref[...], wf_ref[...])
    gate_g = jax.nn.sigmoid(mg_ref[...].astype(F32) + bm_ref[0:1, :])
    gate_f = jax.nn.sigmoid(mf_ref[...].astype(F32) + bm_ref[1:2, :])
    o_ref[...] = (gate_g * y_gla + gate_f * y_fox).astype(BF16)


def _merge(o_gla, o_fox, w_br_gla, w_br_fox, p, b_merge, *, tm, tn, mg_off, mf_off):
    n, d = o_gla.shape
    return pl.pallas_call(
        _merge_kernel,
        grid=(n // tm, d // tn),
        in_specs=[
            pl.BlockSpec((tm, d), lambda i, j: (i, 0)),
            pl.BlockSpec((tm, d), lambda i, j: (i, 0)),
            pl.BlockSpec((d, tn), lambda i, j: (0, j)),
            pl.BlockSpec((d, tn), lambda i, j: (0, j)),
            pl.BlockSpec((tm, tn), lambda i, j: (i, mg_off // tn + j)),
            pl.BlockSpec((tm, tn), lambda i, j: (i, mf_off // tn + j)),
            pl.BlockSpec((2, tn), lambda i, j: (0, j)),
        ],
        out_specs=pl.BlockSpec((tm, tn), lambda i, j: (i, j)),
        out_shape=jax.ShapeDtypeStruct((n, d), BF16),
        compiler_params=_cparams(("parallel", "arbitrary")),
        name="merge",
    )(o_gla, o_fox, w_br_gla, w_br_fox, p, p, b_merge)


def _outproj_kernel(x_ref, m_ref, w_ref, o_ref):
    o_ref[...] = x_ref[...] + _dot(m_ref[...], w_ref[...])


def _outproj(x1, merged, w_out, *, tm, tn):
    n, d = x1.shape
    return pl.pallas_call(
        _outproj_kernel,
        grid=(n // tm, d // tn),
        in_specs=[
            pl.BlockSpec((tm, tn), lambda i, j: (i, j)),
            pl.BlockSpec((tm, d), lambda i, j: (i, 0)),
            pl.BlockSpec((d, tn), lambda i, j: (0, j)),
        ],
        out_specs=pl.BlockSpec((tm, tn), lambda i, j: (i, j)),
        out_shape=jax.ShapeDtypeStruct((n, d), F32),
        compiler_params=_cparams(("parallel", "arbitrary")),
        name="outproj",
    )(x1, merged, w_out)


def kernel(x, norm_ffn1, ffn1_w_in, ffn1_w_out, norm_mix, w_in, gla_w_a_up, gla_b_a, gla_onorm,
           fox_b_f, fox_q_norm, fox_k_norm, w_br_gla, w_br_fox, b_merge, w_out, norm_ffn2,
           ffn2_w_in, ffn2_w_out, norm_final):
    batch, seq, d = x.shape
    n = batch * seq
    depth = ffn1_w_in.shape[0]
    gla_key = gla_w_a_up.shape[-1]
    gla_val = gla_onorm.shape[-1] * GLA_HEADS
    dk, dv = gla_key // GLA_HEADS, gla_val // GLA_HEADS
    fox_heads = fox_b_f.shape[-1]
    fox_w = fox_heads * FOX_DH
    rank = GLA_GATE_RANK

    sizes = (gla_key, gla_key, gla_val, gla_val, rank, fox_w, fox_w, fox_w, fox_heads, fox_w, d, d)
    offs = [0]
    for s in sizes:
        offs.append(offs[-1] + s)
    wide = [i for i, s in enumerate(sizes) if s % LANES == 0]
    new_off, acc = {}, 0
    for i in wide:
        new_off[i] = acc
        acc += sizes[i]

    tm = min(512, n)
    xf = x.reshape(n, d)
    for l in range(depth):
        w_l = w_in[l]
        w_big = jnp.concatenate([w_l[:, offs[i]:offs[i + 1]] for i in wide], axis=1).astype(BF16)
        w_small = jnp.concatenate([w_l[:, offs[4]:offs[5]], w_l[:, offs[8]:offs[9]]], axis=1)
        w_small_pad = jnp.pad(w_small, ((0, 0), (0, LANES - w_small.shape[1]))).astype(BF16)
        w_small_t = w_small.T.astype(BF16)
        w_up_pad = jnp.pad(gla_w_a_up[l], ((0, LANES - rank), (0, 0))).astype(BF16)

        x1, h = _ffn(xf, norm_ffn1[l], ffn1_w_in[l].astype(BF16), ffn1_w_out[l].astype(BF16),
                     norm_mix[l], final=False, tm=tm, tf=512)
        p, s_row, s_t = _proj(h, w_big, w_small_pad, w_small_t, fox_q_norm[l], fox_k_norm[l],
                              tm=tm, tn=1024, q_off=new_off[5], k_off=new_off[6], sec_w=fox_w)
        f_cum = _fcum(s_t, fox_b_f[l], batch=batch, seq=seq, tb=min(512, seq))
        o_gla = _gla(p, s_row, w_up_pad, gla_b_a[l], gla_onorm[l], batch=batch, seq=seq,
                     rows=min(512, seq), q_off=new_off[0], k_off=new_off[1], v_off=new_off[2],
                     r_off=new_off[3], dk=dk, dv=dv)
        o_fox = _fox(p, f_cum, batch=batch, seq=seq, heads=fox_heads, tq=min(256, seq),
                     q_off=new_off[5], k_off=new_off[6], v_off=new_off[7], og_off=new_off[9])
        merged = _merge(o_gla, o_fox, w_br_gla[l].astype(BF16), w_br_fox[l].astype(BF16), p,
                        b_merge[l], tm=tm, tn=512, mg_off=new_off[10], mf_off=new_off[11])
        x2 = _outproj(x1, merged, w_out[l].astype(BF16), tm=tm, tn=512)
        if l == depth - 1:
            xf = _ffn(x2, norm_ffn2[l], ffn2_w_in[l].astype(BF16), ffn2_w_out[l].astype(BF16),
                      norm_final, final=True, tm=tm, tf=512)
        else:
            xf, _ = _ffn(x2, norm_ffn2[l], ffn2_w_in[l].astype(BF16),
                         ffn2_w_out[l].astype(BF16), norm_final, final=False, tm=tm, tf=512)
    return xf.reshape(batch, seq, d)
```

```python
import functools

import jax
import jax.numpy as jnp
from jax import lax
from jax.experimental import pallas as pl
from jax.experimental.pallas import tpu as pltpu

F32 = jnp.float32
BF16 = jnp.bfloat16

EPS = 1e-6
CHUNK = 64
GLA_HEADS = 4
GLA_GATE_RANK = 16
GLA_GATE_NORMALIZER = 16.0
FOX_DH = 128
LANES = 128
LOG2E = 1.4426950408889634
VMEM_LIMIT = 56 * 1024 * 1024


def _cparams(sem):
    return pltpu.CompilerParams(dimension_semantics=sem, vmem_limit_bytes=VMEM_LIMIT)


def _rms(x, g):
    return x * lax.rsqrt(jnp.mean(x * x, axis=-1, keepdims=True) + EPS) * g


def _log_sigmoid(x):
    return jnp.minimum(x, 0.0) - jnp.log1p(jnp.exp(-jnp.abs(x)))


def _split3(x):
    hi = x.astype(BF16)
    r1 = x - hi.astype(F32)
    mid = r1.astype(BF16)
    lo = (r1 - mid.astype(F32)).astype(BF16)
    return hi, mid, lo


def _dot(a, b):
    return jnp.dot(a, b, preferred_element_type=F32)


def _dot_nt(a, b):
    return lax.dot_general(a, b, (((1,), (1,)), ((), ())), preferred_element_type=F32)


def _dot_tn(a, b):
    return lax.dot_general(a, b, (((0,), (0,)), ((), ())), preferred_element_type=F32)


def _ffn_kernel(x_ref, g_ref, wg_ref, wu_ref, wo_ref, gpost_ref, *refs, final):
    if final:
        o_ref, h_ref = refs
    else:
        o_ref, hn_ref, h_ref = refs
    f = pl.program_id(1)

    @pl.when(f == 0)
    def _():
        xv = x_ref[...]
        h_ref[...] = _rms(xv, g_ref[...]).astype(BF16)
        o_ref[...] = xv

    h = h_ref[...]
    g = _dot(h, wg_ref[...])
    u = _dot(h, wu_ref[...])
    a = (g * jax.nn.sigmoid(g) * u * 0.5).astype(BF16)
    o_ref[...] += _dot(a, wo_ref[...])

    @pl.when(f == pl.num_programs(1) - 1)
    def _():
        y = _rms(o_ref[...], gpost_ref[...])
        if final:
            o_ref[...] = y
        else:
            hn_ref[...] = y.astype(BF16)


def _ffn(x, g_pre, w_in, w_out, g_post, *, final, tm, tf):
    n, d = x.shape
    dff = w_out.shape[0]
    nf = dff // tf
    out_shape = [jax.ShapeDtypeStruct((n, d), F32)]
    out_specs = [pl.BlockSpec((tm, d), lambda i, f: (i, 0))]
    if not final:
        out_shape.append(jax.ShapeDtypeStruct((n, d), BF16))
        out_specs.append(pl.BlockSpec((tm, d), lambda i, f: (i, 0)))
    res = pl.pallas_call(
        functools.partial(_ffn_kernel, final=final),
        grid=(n // tm, nf),
        in_specs=[
            pl.BlockSpec((tm, d), lambda i, f: (i, 0)),
            pl.BlockSpec((1, d), lambda i, f: (0, 0)),
            pl.BlockSpec((d, tf), lambda i, f: (0, f)),
            pl.BlockSpec((d, tf), lambda i, f: (0, f + nf)),
            pl.BlockSpec((tf, d), lambda i, f: (f, 0)),
            pl.BlockSpec((1, d), lambda i, f: (0, 0)),
        ],
        out_specs=out_specs,
        out_shape=out_shape,
        scratch_shapes=[pltpu.VMEM((tm, d), BF16)],
        compiler_params=_cparams(("parallel", "arbitrary")),
        name="ffn_final" if final else "ffn_mid",
    )(x, g_pre.reshape(1, d), w_in, w_in, w_out, g_post.reshape(1, d))
    return res[0] if final else (res[0], res[1])


def _proj_kernel(h_ref, w_ref, ws_ref, wst_ref, gq_ref, gk_ref, p_ref, s_ref, st_ref,
                 *, tn, jq, jk, nj_sec, qscale):
    j = pl.program_id(1)
    h = h_ref[...]

    @pl.when(j == 0)
    def _():
        s_ref[...] = _dot(h, ws_ref[...])
        st_ref[...] = _dot_nt(wst_ref[...], h)

    acc = _dot(h, w_ref[...])
    is_q = jnp.logical_and(j >= jq, j < jq + nj_sec)
    is_k = jnp.logical_and(j >= jk, j < jk + nj_sec)

    def head_norm(gain, scale):
        for s in range(tn // FOX_DH):
            blk = acc[:, s * FOX_DH:(s + 1) * FOX_DH]
            y = _rms(blk, gain)
            if scale is not None:
                y = y * scale
            p_ref[:, s * FOX_DH:(s + 1) * FOX_DH] = y.astype(BF16)

    @pl.when(is_q)
    def _():
        head_norm(gq_ref[...], qscale)

    @pl.when(is_k)
    def _():
        head_norm(gk_ref[...], None)

    @pl.when(jnp.logical_not(jnp.logical_or(is_q, is_k)))
    def _():
        p_ref[...] = acc.astype(BF16)


def _proj(h, w_big, w_small, w_small_t, g_q, g_k, *, tm, tn, q_off, k_off, sec_w):
    n, d = h.shape
    width = w_big.shape[1]
    return pl.pallas_call(
        functools.partial(_proj_kernel, tn=tn, jq=q_off // tn, jk=k_off // tn,
                          nj_sec=sec_w // tn, qscale=FOX_DH ** -0.5 * LOG2E),
        grid=(n // tm, width // tn),
        in_specs=[
            pl.BlockSpec((tm, d), lambda i, j: (i, 0)),
            pl.BlockSpec((d, tn), lambda i, j: (0, j)),
            pl.BlockSpec((d, LANES), lambda i, j: (0, 0)),
            pl.BlockSpec((2 * GLA_GATE_RANK, d), lambda i, j: (0, 0)),
            pl.BlockSpec((1, FOX_DH), lambda i, j: (0, 0)),
            pl.BlockSpec((1, FOX_DH), lambda i, j: (0, 0)),
        ],
        out_specs=[
            pl.BlockSpec((tm, tn), lambda i, j: (i, j)),
            pl.BlockSpec((tm, LANES), lambda i, j: (i, 0)),
            pl.BlockSpec((2 * GLA_GATE_RANK, tm), lambda i, j: (0, i)),
        ],
        out_shape=[
            jax.ShapeDtypeStruct((n, width), BF16),
            jax.ShapeDtypeStruct((n, LANES), F32),
            jax.ShapeDtypeStruct((2 * GLA_GATE_RANK, n), F32),
        ],
        compiler_params=_cparams(("parallel", "arbitrary")),
        name="proj",
    )(h, w_big, w_small, w_small_t, g_q.reshape(1, FOX_DH), g_k.reshape(1, FOX_DH))


def _fcum_kernel(st_ref, bf_ref, f_ref, carry_ref, *, nh, tb):
    t = pl.program_id(1)

    @pl.when(t == 0)
    def _():
        carry_ref[...] = jnp.zeros_like(carry_ref)

    lf = _log_sigmoid(st_ref[GLA_GATE_RANK:GLA_GATE_RANK + nh, :] + bf_ref[...]) * LOG2E
    row = lax.broadcasted_iota(jnp.int32, (tb, tb), 0)
    col = lax.broadcasted_iota(jnp.int32, (tb, tb), 1)
    upper = jnp.where(row <= col, 1.0, 0.0).astype(BF16)
    hi, mid, lo = _split3(lf)
    fblk = _dot(hi, upper) + _dot(mid, upper) + _dot(lo, upper) + carry_ref[...]
    f_ref[...] = fblk
    carry_ref[...] = fblk[:, tb - 1:tb]


def _fcum(s_t, b_f, *, batch, seq, tb):
    nh = b_f.shape[0]
    nt = seq // tb
    return pl.pallas_call(
        functools.partial(_fcum_kernel, nh=nh, tb=tb),
        grid=(batch, nt),
        in_specs=[
            pl.BlockSpec((s_t.shape[0], tb), lambda b, t: (0, b * nt + t)),
            pl.BlockSpec((nh, 1), lambda b, t: (0, 0)),
        ],
        out_specs=pl.BlockSpec((nh, tb), lambda b, t: (0, b * nt + t)),
        out_shape=jax.ShapeDtypeStruct((nh, batch * seq), F32),
        scratch_shapes=[pltpu.VMEM((nh, 1), F32)],
        compiler_params=_cparams(("parallel", "arbitrary")),
        name="fcum",
    )(s_t, b_f.reshape(nh, 1))


def _gla_kernel(q_ref, k_ref, v_ref, r_ref, s_ref, wup_ref, ba_ref, gon_ref, o_ref, st_ref,
                *, rows, dk):
    t = pl.program_id(2)

    @pl.when(t == 0)
    def _():
        st_ref[...] = jnp.zeros_like(st_ref)

    z = _dot(s_ref[...].astype(BF16), wup_ref[...]) + ba_ref[...]
    log_a = _log_sigmoid(z) / GLA_GATE_NORMALIZER
    row = lax.broadcasted_iota(jnp.int32, (rows, rows), 0)
    col = lax.broadcasted_iota(jnp.int32, (rows, rows), 1)
    same_chunk = (row // CHUNK) == (col // CHUNK)
    lower = jnp.where(jnp.logical_and(same_chunk, col <= row), 1.0, 0.0).astype(BF16)
    hi, mid, lo = _split3(log_a)
    gcum = _dot(lower, hi) + _dot(lower, mid) + _dot(lower, lo)

    gain = gon_ref[...]
    for c in range(rows // CHUNK):
        r0, r1 = c * CHUNK, (c + 1) * CHUNK
        g_c = gcum[r0:r1]
        g_end = gcum[r1 - 1:r1]
        k_dec = (k_ref[r0:r1, :].astype(F32) * jnp.exp(g_end - g_c)).astype(BF16)
        st = st_ref[...] * jnp.exp(g_end) + _dot_tn(v_ref[r0:r1, :], k_dec)
        st_ref[...] = st
        o = _dot_nt(q_ref[r0:r1, :], st.astype(BF16)) * (dk ** -0.5)
        o = _rms(o, gain)
        rv = r_ref[r0:r1, :].astype(F32)
        o_ref[r0:r1, :] = (o * (rv * jax.nn.sigmoid(rv))).astype(BF16)


def _gla(p, s_row, w_up_pad, b_a, g_onorm, *, batch, seq, rows, q_off, k_off, v_off, r_off,
         dk, dv):
    n = p.shape[0]
    nt = seq // rows
    row_ix = lambda b, h, t: b * nt + t
    return pl.pallas_call(
        functools.partial(_gla_kernel, rows=rows, dk=dk),
        grid=(batch, GLA_HEADS, nt),
        in_specs=[
            pl.BlockSpec((rows, dk), lambda b, h, t: (row_ix(b, h, t), q_off // dk + h)),
            pl.BlockSpec((rows, dk), lambda b, h, t: (row_ix(b, h, t), k_off // dk + h)),
            pl.BlockSpec((rows, dv), lambda b, h, t: (row_ix(b, h, t), v_off // dv + h)),
            pl.BlockSpec((rows, dv), lambda b, h, t: (row_ix(b, h, t), r_off // dv + h)),
            pl.BlockSpec((rows, LANES), lambda b, h, t: (row_ix(b, h, t), 0)),
            pl.BlockSpec((LANES, dk), lambda b, h, t: (0, h)),
            pl.BlockSpec((1, dk), lambda b, h, t: (0, h)),
            pl.BlockSpec((None, 1, dv), lambda b, h, t: (h, 0, 0)),
        ],
        out_specs=pl.BlockSpec((rows, dv), lambda b, h, t: (row_ix(b, h, t), h)),
        out_shape=jax.ShapeDtypeStruct((n, GLA_HEADS * dv), BF16),
        scratch_shapes=[pltpu.VMEM((dv, dk), F32)],
        compiler_params=_cparams(("parallel", "parallel", "arbitrary")),
        name="gla",
    )(p, p, p, p, s_row, w_up_pad, b_a.reshape(1, -1), g_onorm.reshape(GLA_HEADS, 1, dv))


def _fox_kernel(q_ref, k_ref, v_ref, og_ref, f_ref, o_ref, *, tq, tk, hpb):
    i = pl.program_id(2)
    qs = pl.multiple_of(i * tq, tq)
    nfull = qs // tk
    f_q0 = [f_ref[hh, :, pl.ds(qs, tq)][:, 0:1] for hh in range(hpb)]

    def step(ks, carry, masked):
        out = []
        for hh in range(hpb):
            hs = slice(hh * FOX_DH, (hh + 1) * FOX_DH)
            m, l, acc = carry[hh]
            s = _dot_nt(q_ref[:, hs], k_ref[pl.ds(ks, tk), hs])
            s = s + (f_q0[hh] - f_ref[hh, :, pl.ds(ks, tk)])
            if masked:
                row = qs + lax.broadcasted_iota(jnp.int32, (tq, tk), 0)
                col = ks + lax.broadcasted_iota(jnp.int32, (tq, tk), 1)
                s = jnp.where(col <= row, s, -jnp.inf)
            m_new = jnp.maximum(m, jnp.max(s, axis=-1, keepdims=True))
            alpha = jnp.exp2(m - m_new)
            pr = jnp.exp2(s - m_new)
            l = alpha * l + jnp.sum(pr, axis=-1, keepdims=True)
            acc = alpha * acc + _dot(pr.astype(BF16), v_ref[pl.ds(ks, tk), hs])
            out.append((m_new, l, acc))
        return tuple(out)

    init = tuple((jnp.full((tq, 1), -jnp.inf, F32), jnp.zeros((tq, 1), F32),
                  jnp.zeros((tq, FOX_DH), F32)) for _ in range(hpb))
    carry = lax.fori_loop(
        0, nfull, lambda kb, c: step(pl.multiple_of(kb * tk, tk), c, False), init)
    carry = step(pl.multiple_of(nfull * tk, tk), carry, True)
    for hh in range(hpb):
        hs = slice(hh * FOX_DH, (hh + 1) * FOX_DH)
        m, l, acc = carry[hh]
        og = og_ref[:, hs].astype(F32)
        o_ref[:, hs] = (acc / l * jax.nn.sigmoid(og)).astype(BF16)


def _fox(p, f_cum, *, batch, seq, heads, tq, tk, hpb, q_off, k_off, v_off, og_off):
    n = p.shape[0]
    nq = seq // tq
    wb = hpb * FOX_DH
    f3 = f_cum.reshape(heads, 1, n)
    return pl.pallas_call(
        functools.partial(_fox_kernel, tq=tq, tk=tk, hpb=hpb),
        grid=(batch, heads // hpb, nq),
        in_specs=[
            pl.BlockSpec((tq, wb), lambda b, h, i: (b * nq + i, q_off // wb + h)),
            pl.BlockSpec((seq, wb), lambda b, h, i: (b, k_off // wb + h)),
            pl.BlockSpec((seq, wb), lambda b, h, i: (b, v_off // wb + h)),
            pl.BlockSpec((tq, wb), lambda b, h, i: (b * nq + i, og_off // wb + h)),
            pl.BlockSpec((hpb, 1, seq), lambda b, h, i: (h, 0, b)),
        ],
        out_specs=pl.BlockSpec((tq, wb), lambda b, h, i: (b * nq + i, h)),
        out_shape=jax.ShapeDtypeStruct((n, heads * FOX_DH), BF16),
        compiler_params=_cparams(("parallel", "parallel", "arbitrary")),
        name="fox",
    )(p, p, p, p, f3)


def _merge_kernel(yg_ref, yf_ref, wg_ref, wf_ref, mg_ref, mf_ref, bm_ref, o_ref):
    y_gla = _dot(yg_ref[...], wg_ref[...])
    y_fox = _dot(yf_ref[...], wf_ref[...])
    gate_g = jax.nn.sigmoid(mg_ref[...].astype(F32) + bm_ref[0:1, :])
    gate_f = jax.nn.sigmoid(mf_ref[...].astype(F32) + bm_ref[1:2, :])
    o_ref[...] = (gate_g * y_gla + gate_f * y_fox).astype(BF16)


def _merge(o_gla, o_fox, w_br_gla, w_br_fox, p, b_merge, *, tm, tn, mg_off, mf_off):
    n, d = o_gla.shape
    return pl.pallas_call(
        _merge_kernel,
        grid=(n // tm, d // tn),
        in_specs=[
            pl.BlockSpec((tm, d), lambda i, j: (i, 0)),
            pl.BlockSpec((tm, d), lambda i, j: (i, 0)),
            pl.BlockSpec((d, tn), lambda i, j: (0, j)),
            pl.BlockSpec((d, tn), lambda i, j: (0, j)),
            pl.BlockSpec((tm, tn), lambda i, j: (i, mg_off // tn + j)),
            pl.BlockSpec((tm, tn), lambda i, j: (i, mf_off // tn + j)),
            pl.BlockSpec((2, tn), lambda i, j: (0, j)),
        ],
        out_specs=pl.BlockSpec((tm, tn), lambda i, j: (i, j)),
        out_shape=jax.ShapeDtypeStruct((n, d), BF16),
        compiler_params=_cparams(("parallel", "arbitrary")),
        name="merge",
    )(o_gla, o_fox, w_br_gla, w_br_fox, p, p, b_merge)


def _outproj_kernel(x_ref, m_ref, w_ref, o_ref):
    o_ref[...] = x_ref[...] + _dot(m_ref[...], w_ref[...])


def _outproj(x1, merged, w_out, *, tm, tn):
    n, d = x1.shape
    return pl.pallas_call(
        _outproj_kernel,
        grid=(n // tm, d // tn),
        in_specs=[
            pl.BlockSpec((tm, tn), lambda i, j: (i, j)),
            pl.BlockSpec((tm, d), lambda i, j: (i, 0)),
            pl.BlockSpec((d, tn), lambda i, j: (0, j)),
        ],
        out_specs=pl.BlockSpec((tm, tn), lambda i, j: (i, j)),
        out_shape=jax.ShapeDtypeStruct((n, d), F32),
        compiler_params=_cparams(("parallel", "arbitrary")),
        name="outproj",
    )(x1, merged, w_out)


def kernel(x, norm_ffn1, ffn1_w_in, ffn1_w_out, norm_mix, w_in, gla_w_a_up, gla_b_a, gla_onorm,
           fox_b_f, fox_q_norm, fox_k_norm, w_br_gla, w_br_fox, b_merge, w_out, norm_ffn2,
           ffn2_w_in, ffn2_w_out, norm_final):
    batch, seq, d = x.shape
    n = batch * seq
    depth = ffn1_w_in.shape[0]
    gla_key = gla_w_a_up.shape[-1]
    gla_val = gla_onorm.shape[-1] * GLA_HEADS
    dk, dv = gla_key // GLA_HEADS, gla_val // GLA_HEADS
    fox_heads = fox_b_f.shape[-1]
    fox_w = fox_heads * FOX_DH
    rank = GLA_GATE_RANK

    sizes = (gla_key, gla_key, gla_val, gla_val, rank, fox_w, fox_w, fox_w, fox_heads, fox_w, d, d)
    offs = [0]
    for s in sizes:
        offs.append(offs[-1] + s)
    wide = [i for i, s in enumerate(sizes) if s % LANES == 0]
    new_off, acc = {}, 0
    for i in wide:
        new_off[i] = acc
        acc += sizes[i]

    tm = min(512, n)
    xf = x.reshape(n, d)
    for l in range(depth):
        w_l = w_in[l]
        w_big = jnp.concatenate([w_l[:, offs[i]:offs[i + 1]] for i in wide], axis=1).astype(BF16)
        w_small = jnp.concatenate([w_l[:, offs[4]:offs[5]], w_l[:, offs[8]:offs[9]]], axis=1)
        w_small_pad = jnp.pad(w_small, ((0, 0), (0, LANES - w_small.shape[1]))).astype(BF16)
        w_small_t = w_small.T.astype(BF16)
        w_up_pad = jnp.pad(gla_w_a_up[l], ((0, LANES - rank), (0, 0))).astype(BF16)

        x1, h = _ffn(xf, norm_ffn1[l], ffn1_w_in[l].astype(BF16), ffn1_w_out[l].astype(BF16),
                     norm_mix[l], final=False, tm=tm, tf=512)
        p, s_row, s_t = _proj(h, w_big, w_small_pad, w_small_t, fox_q_norm[l], fox_k_norm[l],
                              tm=tm, tn=1024, q_off=new_off[5], k_off=new_off[6], sec_w=fox_w)
        f_cum = _fcum(s_t, fox_b_f[l], batch=batch, seq=seq, tb=min(512, seq))
        o_gla = _gla(p, s_row, w_up_pad, gla_b_a[l], gla_onorm[l], batch=batch, seq=seq,
                     rows=min(512, seq), q_off=new_off[0], k_off=new_off[1], v_off=new_off[2],
                     r_off=new_off[3], dk=dk, dv=dv)
        o_fox = _fox(p, f_cum, batch=batch, seq=seq, heads=fox_heads, tq=min(256, seq),
                     tk=min(1024, seq), hpb=2, q_off=new_off[5], k_off=new_off[6], v_off=new_off[7], og_off=new_off[9])
        merged = _merge(o_gla, o_fox, w_br_gla[l].astype(BF16), w_br_fox[l].astype(BF16), p,
                        b_merge[l], tm=tm, tn=512, mg_off=new_off[10], mf_off=new_off[11])
        x2 = _outproj(x1, merged, w_out[l].astype(BF16), tm=tm, tn=512)
        if l == depth - 1:
            xf = _ffn(x2, norm_ffn2[l], ffn2_w_in[l].astype(BF16), ffn2_w_out[l].astype(BF16),
                      norm_final, final=True, tm=tm, tf=512)
        else:
            xf, _ = _ffn(x2, norm_ffn2[l], ffn2_w_in[l].astype(BF16),
                         ffn2_w_out[l].astype(BF16), norm_final, final=False, tm=tm, tf=512)
    return xf.reshape(batch, seq, d)
```
